```python
import math
import jax
import jax.numpy as jnp
from jax import lax
import numpy as np

D_MODEL = 2048
BATCH = 1
SEQ = 8192
DEPTH = 2

GRID_W = 64
CTX_LEN = 256
DIRECTIONS = (False, True)

D_MIX = D_MODEL
D_RG = D_MIX // 2
RG_HEADS = 16
RG_HEAD_DIM = D_RG // RG_HEADS
RG_CONV = 4
RG_PAD = (RG_CONV // 2, RG_CONV - 1 - RG_CONV // 2)
RG_C = 8.0
D_RW = D_MIX - D_RG
RW_HEAD_DIM = 64
RW_HEADS = D_RW // RW_HEAD_DIM
RW_DECAY_RANK = 64
RW_ICL_RANK = 64
RW_GATE_RANK = 160
RW_SPLITS = (D_RW, 2 * D_RW, 3 * D_RW, 3 * D_RW + 2 * RW_DECAY_RANK,
             3 * D_RW + 2 * RW_DECAY_RANK + 2 * RW_ICL_RANK)
RW_COLS = RW_SPLITS[-1] + RW_GATE_RANK
IN0_COLS = 2 * D_RG + RW_COLS
RW_GN_EPS = 64e-5
D_FF = 5504

D_HY = D_MODEL
HY_ORDER = 2
HY_SHORT = 3
HY_PAD = (HY_SHORT // 2, HY_SHORT // 2)
HY_BANDS = 16
HY_EMB = 1 + 2 * HY_BANDS
HY_HIDDEN = 64
HY_MAX_DECAY = math.log(1e-2) / 0.3
HY_MIN_DECAY = math.log(1e-2) / 1.5
N_EXPERTS = 8
TOP_K = 2
D_EXPERT = 7168

LN_EPS = 1e-5
ALPHA = (2 * DEPTH) ** 0.25
BETA = (8 * DEPTH) ** -0.25

kernel_name = 'hybrid_rglru_rwkv7_hyena_moe_dit'


def _layer_norm(x, g, b):
    xf = x.astype(jnp.float32)
    mu = jnp.mean(xf, -1, keepdims=True)
    var = jnp.mean(jnp.square(xf - mu), -1, keepdims=True)
    return ((xf - mu) * lax.rsqrt(var + LN_EPS) * g + b).astype(x.dtype)


def _ada(cvec, w_mod, b_mod, n):
    m = jax.nn.silu(cvec) @ w_mod + b_mod
    return jnp.split(m[..., None, :], n, axis=-1)


def _modulate(x, shift, scale):
    return x * (1 + scale) + shift


def _dw_conv(x, w, b, pad):
    y = lax.conv_general_dilated(x, w[:, None, :].astype(x.dtype), window_strides=(1,),
                                 padding=[pad], dimension_numbers=('NWC', 'WIO', 'NWC'),
                                 feature_group_count=x.shape[-1])
    return y + b


def _swiglu(u, wg, wu, wd):
    return (jax.nn.silu(u @ wg) * (u @ wu)) @ wd


def _block_diag(x, w, b):
    xh = x.reshape(x.shape[:-1] + (RG_HEADS, RG_HEAD_DIM))
    return jnp.einsum('blhi,hij->blhj', xh, w).reshape(x.shape) + b


def _rglru_coeffs(xc, w_r, b_r, w_i, b_i, lam):
    f32 = jnp.float32
    r = jax.nn.sigmoid(_block_diag(xc, w_r, b_r).astype(f32))
    i = jax.nn.sigmoid(_block_diag(xc, w_i, b_i).astype(f32))
    log_a = RG_C * r * jax.nn.log_sigmoid(lam.astype(f32))
    bx = jnp.sqrt(-jnp.expm1(2.0 * log_a)) * i * xc.astype(f32)
    return jnp.exp(log_a), bx


def _linear_scan(a, bx, h0, reverse):
    def combine(first, second):
        a1, b1 = first
        a2, b2 = second
        return a1 * a2, a2 * b1 + b2
    cum_a, cum_b = lax.associative_scan(combine, (a, bx), reverse=reverse, axis=1)
    return cum_b + cum_a * h0[:, None, :]


def _heads(z):
    return z.reshape(z.shape[:-1] + (RW_HEADS, RW_HEAD_DIM))


def _grid_shift(p):
    b_, l_, ch = p.shape
    g = p.reshape(b_, l_ // GRID_W, GRID_W, ch)
    q = ch // 4
    left = jnp.pad(g[:, :, :-1, :q], ((0, 0), (0, 0), (1, 0), (0, 0)))
    right = jnp.pad(g[:, :, 1:, q:2 * q], ((0, 0), (0, 0), (0, 1), (0, 0)))
    up = jnp.pad(g[:, :-1, :, 2 * q:3 * q], ((0, 0), (1, 0), (0, 0), (0, 0)))
    down = jnp.pad(g[:, 1:, :, 3 * q:], ((0, 0), (0, 1), (0, 0), (0, 0)))
    return jnp.concatenate([left, right, up, down], axis=-1).reshape(b_, l_, ch)


def _seq_shift(p):
    h = p.shape[-1] // 2
    prev = jnp.pad(p[:, :-1, :h], ((0, 0), (1, 0), (0, 0)))
    nxt = jnp.pad(p[:, 1:, h:], ((0, 0), (0, 1), (0, 0)))
    return jnp.concatenate([prev, nxt], axis=-1)


def _rw_keynorm(k, k_k):
    kh = _heads((k * k_k).astype(jnp.float32))
    return kh / jnp.maximum(jnp.sqrt(jnp.sum(kh * kh, -1, keepdims=True)), 1e-12)


def _rw_direction(k, kk, wlo, alo, w0, w2, a0, a2, k_a):
    f32 = jnp.float32
    log_w = -jax.nn.softplus(-(w0 + jnp.tanh(wlo) @ w2).astype(f32)) - 0.5
    decay = jnp.exp(-jnp.exp(log_w))
    icl = jax.nn.sigmoid((a0 + alo @ a2).astype(f32))
    k_d = k.astype(f32) * (1.0 + (icl - 1.0) * k_a)
    return _heads(decay), _heads(k_d), -kk, kk * _heads(icl)


def _wkv7_scan(s0, decay, k, v, a, b, r, reverse):
    emit = r is not None
    seq = (decay, k, v, a, b) + ((r,) if emit else ())
    xs = tuple(jnp.moveaxis(z, 1, 0) for z in seq)

    def step(s, inp):
        w_t, k_t, v_t, a_t, b_t = inp[:5]
        sa = jnp.einsum('bhvk,bhk->bhv', s, a_t)
        s = s * w_t[:, :, None, :] + sa[..., None] * b_t[:, :, None, :] + v_t[..., None] * k_t[:, :, None, :]
        y = jnp.einsum('bhvk,bhk->bhv', s, inp[5]) if emit else None
        return s, y

    s_fin, ys = lax.scan(step, s0, xs, reverse=reverse)
    return s_fin, (jnp.moveaxis(ys, 0, 1) if emit else None)


def _even_mixer(u, u_ctx, in_w, rg_conv_w, rg_conv_b, rg_wr, rg_br, rg_wi, rg_bi, rg_lambda,
                rw_mu, rw_w0, rw_w2, rw_a0, rw_a2, rw_g2, rw_kk, rw_ka, rw_rk, rw_lnx_g, rw_lnx_b, out_w):
    f32 = jnp.float32
    b_, l_, _ = u.shape
    p = u @ in_w
    pc = u_ctx @ in_w
    xa, ga, pb = jnp.split(p, [D_RG, 2 * D_RG], axis=-1)
    xa_c, pb_c = pc[..., :D_RG], pc[..., 2 * D_RG:]

    xa = _dw_conv(xa, rg_conv_w, rg_conv_b, RG_PAD)
    xa_c = _dw_conv(xa_c, rg_conv_w, rg_conv_b, RG_PAD)
    h_sum = jnp.zeros((b_, l_, D_RG), f32)
    for d, rev in enumerate(DIRECTIONS):
        a_c, bx_c = _rglru_coeffs(xa_c, rg_wr[d], rg_br[d], rg_wi[d], rg_bi[d], rg_lambda[d])
        h_c = _linear_scan(a_c, bx_c, jnp.zeros((b_, D_RG), f32), rev)
        h0 = h_c[:, 0] if rev else h_c[:, -1]
        a_l, bx_l = _rglru_coeffs(xa, rg_wr[d], rg_br[d], rg_wi[d], rg_bi[d], rg_lambda[d])
        h_sum = h_sum + _linear_scan(a_l, bx_l, h0, rev)
    y_a = h_sum * jax.nn.gelu(ga.astype(f32))

    pb = pb + (_grid_shift(pb) - pb) * rw_mu
    pb_c = pb_c + (_seq_shift(pb_c) - pb_c) * rw_mu
    r, k, v, wlo, alo, glo = jnp.split(pb, RW_SPLITS, axis=-1)
    _, k_c, v_c, wlo_c, alo_c, _ = jnp.split(pb_c, RW_SPLITS, axis=-1)
    kk, kk_c = _rw_keynorm(k, rw_kk), _rw_keynorm(k_c, rw_kk)
    vh, vh_c = _heads(v.astype(f32)), _heads(v_c.astype(f32))
    rh = _heads(r.astype(f32))
    s0 = jnp.zeros((b_, RW_HEADS, RW_HEAD_DIM, RW_HEAD_DIM), f32)
    y_sum = jnp.zeros((b_, l_, RW_HEADS, RW_HEAD_DIM), f32)
    for d, rev in enumerate(DIRECTIONS):
        wsl = slice(d * RW_DECAY_RANK, (d + 1) * RW_DECAY_RANK)
        asl = slice(d * RW_ICL_RANK, (d + 1) * RW_ICL_RANK)
        dec_c, kd_c, av_c, bv_c = _rw_direction(k_c, kk_c, wlo_c[..., wsl], alo_c[..., asl],
                                                rw_w0[d], rw_w2[d], rw_a0[d], rw_a2[d], rw_ka)
        s_ctx, _ = _wkv7_scan(s0, dec_c, kd_c, vh_c, av_c, bv_c, None, rev)
        dec, kd, av, bv = _rw_direction(k, kk, wlo[..., wsl], alo[..., asl],
                                        rw_w0[d], rw_w2[d], rw_a0[d], rw_a2[d], rw_ka)
        _, y_d = _wkv7_scan(s_ctx, dec, kd, vh, av, bv, rh, rev)
        y_sum = y_sum + y_d
    mu = jnp.mean(y_sum, -1, keepdims=True)
    var = jnp.mean(jnp.square(y_sum - mu), -1, keepdims=True)
    y_gn = ((y_sum - mu) * lax.rsqrt(var + RW_GN_EPS)).reshape(b_, l_, D_RW) * rw_lnx_g + rw_lnx_b
    bonus = jnp.sum(rh * _heads(k.astype(f32)) * rw_rk, -1, keepdims=True) * vh
    g = jax.nn.sigmoid(glo) @ rw_g2
    y_b = (y_gn + bonus.reshape(b_, l_, D_RW)) * g

    return jnp.concatenate([y_a, y_b], axis=-1).astype(u.dtype) @ out_w


def _hyena_filters(l_, flt_w1, flt_b1, flt_w2, flt_b2, flt_w3, flt_b3, flt_w4, flt_freq):
    f32 = jnp.float32
    t = jnp.linspace(0.0, 1.0, l_, dtype=f32)[:, None]
    w = (2.0 * math.pi / l_) * jnp.arange(l_, dtype=f32)[:, None]
    bands = jnp.linspace(1e-4, HY_BANDS - 1, HY_BANDS, dtype=f32)[None, :]
    z = jnp.concatenate([t, jnp.cos(bands * w), -jnp.sin(bands * w)], axis=-1)
    h = jnp.sin(flt_freq * (z @ flt_w1 + flt_b1))
    h = jnp.sin(flt_freq * (h @ flt_w2 + flt_b2))
    h = jnp.sin(flt_freq * (h @ flt_w3 + flt_b3))
    k = (h @ flt_w4).astype(f32).reshape(l_, HY_ORDER, 2, D_HY)
    deltas = jnp.linspace(HY_MIN_DECAY, HY_MAX_DECAY, D_HY, dtype=f32)
    k = k * jnp.exp(-t * jnp.abs(deltas))[:, None, None, :]
    kf, kb = k[:, :, 0], k[:, :, 1]
    k2 = jnp.concatenate([kf, jnp.zeros_like(kf[:1]), kb[:0:-1]], axis=0)
    return jnp.fft.rfft(k2, axis=0)


def _long_conv(z, k_freq, l_):
    zf = jnp.fft.rfft(z, n=2 * l_, axis=1)
    return jnp.fft.irfft(zf * k_freq[None], n=2 * l_, axis=1)[:, :l_]


def _hyena_mixer(u, in_w, conv_w, conv_b, flt_w1, flt_b1, flt_w2, flt_b2, flt_w3, flt_b3,
                 flt_w4, flt_freq, hy_bias, out_w):
    f32 = jnp.float32
    l_ = u.shape[1]
    p = _dw_conv(u @ in_w, conv_w, conv_b, HY_PAD)
    v, x1, x2 = jnp.split(p, 3, axis=-1)
    k_freq = _hyena_filters(l_, flt_w1, flt_b1, flt_w2, flt_b2, flt_w3, flt_b3, flt_w4, flt_freq)
    z = v.astype(f32)
    for o, gate in enumerate((x1, x2)):
        z = gate.astype(f32) * (_long_conv(z, k_freq[:, o], l_) + hy_bias[o] * z)
    return z.astype(u.dtype) @ out_w


def _moe_swiglu(u, router_w, router_b, wg, wu, wd):
    logits = (u @ router_w + router_b).astype(jnp.float32)
    top_val, top_idx = lax.top_k(logits, TOP_K)
    weights = jax.nn.softmax(top_val, axis=-1)
    gates = jnp.sum(jax.nn.one_hot(top_idx, N_EXPERTS, dtype=jnp.float32) * weights[..., None], axis=-2)
    y = jnp.zeros(u.shape, jnp.float32)
    for e in range(N_EXPERTS):
        y = y + gates[..., e:e + 1] * _swiglu(u, wg[e], wu[e], wd[e])
    return y.astype(u.dtype)


def setup_inputs(seed: int = 0) -> dict:
    key = jax.random.key(seed)
    ks = iter(jax.random.split(key, 64))
    f32 = jnp.float32

    def nrm(shape, scale):
        return scale * jax.random.normal(next(ks), shape, f32)

    def gain(shape):
        return 1.0 + nrm(shape, 0.05)

    def unif(shape, lo, hi):
        return jax.random.uniform(next(ks), shape, f32, minval=lo, maxval=hi)

    d = D_MODEL
    inp = {}
    inp['x'] = nrm((BATCH, SEQ, d), 1.0)
    inp['c'] = nrm((BATCH, d), 1.0)
    inp['ctx'] = nrm((BATCH, CTX_LEN, d), 1.0)
    inp['c_ctx'] = nrm((d,), 1.0)
    inp['mod0_w'] = nrm((d, 6 * d), 0.5 * d ** -0.5)
    inp['mod0_b'] = nrm((6 * d,), 0.02)
    inp['in0_w'] = nrm((d, IN0_COLS), d ** -0.5)
    inp['rg_conv_w'] = nrm((RG_CONV, D_RG), RG_CONV ** -0.5)
    inp['rg_conv_b'] = nrm((D_RG,), 0.02)
    inp['rg_wr'] = nrm((2, RG_HEADS, RG_HEAD_DIM, RG_HEAD_DIM), RG_HEAD_DIM ** -0.5)
    inp['rg_br'] = nrm((2, D_RG), 0.02)
    inp['rg_wi'] = nrm((2, RG_HEADS, RG_HEAD_DIM, RG_HEAD_DIM), RG_HEAD_DIM ** -0.5)
    inp['rg_bi'] = nrm((2, D_RG), 0.02)
    a8 = unif((2, D_RG), 0.9, 0.999)
    a_base = a8 ** (1.0 / RG_C)
    inp['rg_lambda'] = jnp.log(a_base) - jnp.log1p(-a_base)
    inp['rw_mu'] = unif((RW_COLS,), 0.0, 1.0)
    inp['rw_w0'] = unif((2, D_RW), -6.0, -1.0)
    inp['rw_w2'] = nrm((2, RW_DECAY_RANK, D_RW), 0.5 * RW_DECAY_RANK ** -0.5)
    inp['rw_a0'] = nrm((2, D_RW), 0.5)
    inp['rw_a2'] = nrm((2, RW_ICL_RANK, D_RW), RW_ICL_RANK ** -0.5)
    inp['rw_g2'] = nrm((RW_GATE_RANK, D_RW), RW_GATE_RANK ** -0.5)
    inp['rw_kk'] = 0.85 + nrm((D_RW,), 0.05)
    inp['rw_ka'] = gain((D_RW,))
    inp['rw_rk'] = nrm((RW_HEADS, RW_HEAD_DIM), 0.1)
    inp['rw_lnx_g'] = gain((D_RW,))
    inp['rw_lnx_b'] = nrm((D_RW,), 0.02)
    inp['out0_w'] = nrm((D_MIX, d), BETA * D_MIX ** -0.5)
    inp['ln0a_g'] = gain((d,))
    inp['ln0a_b'] = nrm((d,), 0.02)
    inp['ffn0_wg'] = nrm((d, D_FF), d ** -0.5)
    inp['ffn0_wu'] = nrm((d, D_FF), d ** -0.5)
    inp['ffn0_wd'] = nrm((D_FF, d), BETA * D_FF ** -0.5)
    inp['ln0b_g'] = gain((d,))
    inp['ln0b_b'] = nrm((d,), 0.02)
    inp['mod1_w'] = nrm((d, 6 * d), 0.5 * d ** -0.5)
    inp['mod1_b'] = nrm((6 * d,), 0.02)
    inp['in1_w'] = nrm((d, 3 * D_HY), d ** -0.5)
    inp['hy_conv_w'] = nrm((HY_SHORT, 3 * D_HY), HY_SHORT ** -0.5)
    inp['hy_conv_b'] = nrm((3 * D_HY,), 0.02)
    inp['flt_w1'] = nrm((HY_EMB, HY_HIDDEN), HY_EMB ** -0.5)
    inp['flt_b1'] = nrm((HY_HIDDEN,), 0.1)
    inp['flt_w2'] = nrm((HY_HIDDEN, HY_HIDDEN), HY_HIDDEN ** -0.5)
    inp['flt_b2'] = nrm((HY_HIDDEN,), 0.1)
    inp['flt_w3'] = nrm((HY_HIDDEN, HY_HIDDEN), HY_HIDDEN ** -0.5)
    inp['flt_b3'] = nrm((HY_HIDDEN,), 0.1)
    inp['flt_w4'] = nrm((HY_HIDDEN, HY_ORDER * 2 * D_HY), 0.25 * HY_HIDDEN ** -0.5)
    inp['flt_freq'] = gain((HY_HIDDEN,))
    inp['hy_bias'] = nrm((HY_ORDER, D_HY), 0.5)
    inp['out1_w'] = nrm((D_HY, d), BETA * D_HY ** -0.5)
    inp['ln1a_g'] = gain((d,))
    inp['ln1a_b'] = nrm((d,), 0.02)
    inp['router_w'] = nrm((d, N_EXPERTS), d ** -0.5)
    inp['router_b'] = nrm((N_EXPERTS,), 0.01)
    inp['moe_wg'] = nrm((N_EXPERTS, d, D_EXPERT), d ** -0.5)
    inp['moe_wu'] = nrm((N_EXPERTS, d, D_EXPERT), d ** -0.5)
    inp['moe_wd'] = nrm((N_EXPERTS, D_EXPERT, d), BETA * D_EXPERT ** -0.5)
    inp['ln1b_g'] = gain((d,))
    inp['ln1b_b'] = nrm((d,), 0.02)
    return inp


def reference(x, c, ctx, c_ctx, mod0_w, mod0_b, in0_w, rg_conv_w, rg_conv_b, rg_wr, rg_br, rg_wi, rg_bi,
              rg_lambda, rw_mu, rw_w0, rw_w2, rw_a0, rw_a2, rw_g2, rw_kk, rw_ka, rw_rk, rw_lnx_g, rw_lnx_b,
              out0_w, ln0a_g, ln0a_b, ffn0_wg, ffn0_wu, ffn0_wd, ln0b_g, ln0b_b, mod1_w, mod1_b, in1_w,
              hy_conv_w, hy_conv_b, flt_w1, flt_b1, flt_w2, flt_b2, flt_w3, flt_b3, flt_w4, flt_freq, hy_bias,
              out1_w, ln1a_g, ln1a_b, router_w, router_b, moe_wg, moe_wu, moe_wd, ln1b_g, ln1b_b):
    d = D_MODEL
    for layer in range(DEPTH):
        if layer % 2 == 0:
            sh_m, sc_m, g_m, sh_f, sc_f, g_f = _ada(c, mod0_w, mod0_b, 6)
            csh_m, csc_m = _ada(c_ctx, mod0_w[:, :2 * d], mod0_b[:2 * d], 2)
            mix = _even_mixer(_modulate(x, sh_m, sc_m), _modulate(ctx, csh_m, csc_m), in0_w,
                              rg_conv_w, rg_conv_b, rg_wr, rg_br, rg_wi, rg_bi, rg_lambda,
                              rw_mu, rw_w0, rw_w2, rw_a0, rw_a2, rw_g2, rw_kk, rw_ka, rw_rk,
                              rw_lnx_g, rw_lnx_b, out0_w)
            x = _layer_norm(ALPHA * x + g_m * mix, ln0a_g, ln0a_b)
            ffn = _swiglu(_modulate(x, sh_f, sc_f), ffn0_wg, ffn0_wu, ffn0_wd)
            x = _layer_norm(ALPHA * x + g_f * ffn, ln0b_g, ln0b_b)
        else:
            sh_m, sc_m, g_m, sh_f, sc_f, g_f = _ada(c, mod1_w, mod1_b, 6)
            mix = _hyena_mixer(_modulate(x, sh_m, sc_m), in1_w, hy_conv_w, hy_conv_b, flt_w1, flt_b1,
                               flt_w2, flt_b2, flt_w3, flt_b3, flt_w4, flt_freq, hy_bias, out1_w)
            x = _layer_norm(ALPHA * x + g_m * mix, ln1a_g, ln1a_b)
            ffn = _moe_swiglu(_modulate(x, sh_f, sc_f), router_w, router_b, moe_wg, moe_wu, moe_wd)
            x = _layer_norm(ALPHA * x + g_f * ffn, ln1b_g, ln1b_b)
    return x
```

```python
import functools
import math

import jax
import jax.numpy as jnp
import numpy as np
from jax import lax
from jax.experimental import pallas as pl
from jax.experimental.pallas import tpu as pltpu

F32 = jnp.float32
BF16 = jnp.bfloat16

D_MODEL = 2048
SEQ = 8192
CTX_LEN = 256
GRID_W = 64
DEPTH = 2
ALPHA = (2 * DEPTH) ** 0.25
LN_EPS = 1e-5

D_RG = 1024
RG_HEADS = 16
RG_C = 8.0
D_RW = 1024
RW_HEAD_DIM = 64
RW_RANK = 64
RW_GATE_RANK = 160
RW_COLS = 3 * D_RW + 4 * RW_RANK + RW_GATE_RANK
RW_COLS_PAD = 3584
RW_GN_EPS = 64e-5
D_FF = 5504
D_FF_PAD = 5632

D_HY = 2048
HY_BANDS = 16
HY_EMB = 1 + 2 * HY_BANDS
HY_HIDDEN = 64
HY_MAX_DECAY = math.log(1e-2) / 0.3
HY_MIN_DECAY = math.log(1e-2) / 1.5
N_EXPERTS = 8
D_EXPERT = 7168

SEQ_ALL = CTX_LEN + SEQ
SCAN_BLK = 256
N_SCAN_BLK = SEQ_ALL // SCAN_BLK
WKV_C = 64
N_PAIR = D_RW // 128

VMEM_LIMIT = 56 * 1024 * 1024


def _cp(*sem):
    return pltpu.CompilerParams(dimension_semantics=sem, vmem_limit_bytes=VMEM_LIMIT)


def _dot(a, b):
    return jnp.dot(a, b, preferred_element_type=F32)


def _dot_nt(a, b):
    return lax.dot_general(a, b, (((1,), (1,)), ((), ())), preferred_element_type=F32)


def _dot_tn(a, b):
    return lax.dot_general(a, b, (((0,), (0,)), ((), ())), preferred_element_type=F32)


def _sigmoid(x):
    return 1.0 / (1.0 + jnp.exp(-x))


def _silu(x):
    return x * _sigmoid(x)


def _softplus(y):
    return jnp.maximum(y, 0.0) + jnp.log1p(jnp.exp(-jnp.abs(y)))


def _split_dot(x, e_bf16):
    hi = x.astype(BF16)
    lo = (x - hi.astype(F32)).astype(BF16)
    return _dot(hi, e_bf16) + _dot(lo, e_bf16)


def _ada_kernel(c_ref, w_ref, b_ref, o_ref):
    c = c_ref[...]
    o_ref[...] = _dot(_silu(c).astype(BF16), w_ref[...].astype(BF16)) + b_ref[...]


def _ada(c_rows, w, b):
    n = w.shape[1]
    tn = 1024
    return pl.pallas_call(
        _ada_kernel,
        grid=(n // tn,),
        in_specs=[pl.BlockSpec((8, D_MODEL), lambda j: (0, 0)),
                  pl.BlockSpec((D_MODEL, tn), lambda j: (0, j)),
                  pl.BlockSpec((1, tn), lambda j: (0, j))],
        out_specs=pl.BlockSpec((8, tn), lambda j: (0, j)),
        out_shape=jax.ShapeDtypeStruct((8, n), F32),
        compiler_params=_cp("parallel"),
        name="ada",
    )(c_rows, w, b.reshape(1, n))


def _inproj_kernel(x_ref, sh_ref, sc_ref, w_ref, o_ref, xb_ref, *, n_ctx, tm):
    @pl.when(pl.program_id(1) == 0)
    def _():
        sh = sh_ref[0:1, :]
        sc = sc_ref[0:1, :]
        if n_ctx:
            row = pl.program_id(0) * tm + lax.broadcasted_iota(jnp.int32, (tm, 1), 0)
            is_ctx = row < n_ctx
            sh = jnp.where(is_ctx, sh_ref[1:2, :], sh)
            sc = jnp.where(is_ctx, sc_ref[1:2, :], sc)
        xb_ref[...] = (x_ref[...] * (1.0 + sc) + sh).astype(BF16)

    o_ref[...] = _dot(xb_ref[...], w_ref[...]).astype(o_ref.dtype)


def _inproj(x, mod, w_bf16, *, n_ctx, tm, tn, out_dtype=F32):
    m = x.shape[0]
    n = w_bf16.shape[1]
    return pl.pallas_call(
        functools.partial(_inproj_kernel, n_ctx=n_ctx, tm=tm),
        grid=(m // tm, n // tn),
        in_specs=[pl.BlockSpec((tm, D_MODEL), lambda i, j: (i, 0)),
                  pl.BlockSpec((8, D_MODEL), lambda i, j: (0, 0)),
                  pl.BlockSpec((8, D_MODEL), lambda i, j: (0, 1)),
                  pl.BlockSpec((D_MODEL, tn), lambda i, j: (0, j))],
        out_specs=pl.BlockSpec((tm, tn), lambda i, j: (i, j)),
        out_shape=jax.ShapeDtypeStruct((m, n), out_dtype),
        scratch_shapes=[pltpu.VMEM((tm, D_MODEL), BF16)],
        compiler_params=_cp("parallel", "arbitrary"),
        name="inproj",
    )(x, mod, mod, w_bf16)


def _scan_blk(s, rev):
    if not rev:
        return s
    return jnp.where(s == 0, 0, N_SCAN_BLK - s)


def _lat_blk(s, rev):
    return jnp.maximum(_scan_blk(s, rev) - 1, 0) if not rev else N_SCAN_BLK - 1 - jnp.maximum(s, 1)


def _rglru_kernel(*refs, rev, final):
    if final:
        (xm_ref, xp_ref, xn_ref, cw_ref, cb_ref, wr_ref, br_ref, wi_ref, bi_ref, lam_ref,
         ga_ref, hf_ref, o_ref, ext_ref, a_ref, b_ref, h_ref) = refs
    else:
        (xm_ref, xp_ref, xn_ref, cw_ref, cb_ref, wr_ref, br_ref, wi_ref, bi_ref, lam_ref,
         o_ref, ext_ref, a_ref, b_ref, h_ref) = refs
    s = pl.program_id(0)
    blk = _scan_blk(s, rev)
    seg_first = (blk == 0) | (blk == 1)
    seg_last = (blk == 0) | (blk == N_SCAN_BLK - 1)
    t = SCAN_BLK
    ext_ref[0:8, :] = jnp.where(seg_first, 0.0, xp_ref[...])
    ext_ref[8:8 + t, :] = xm_ref[...]
    ext_ref[8 + t:16 + t, :] = jnp.where(seg_last, 0.0, xn_ref[...])
    xc = (cw_ref[0:1, :] * ext_ref[6:6 + t, :] + cw_ref[1:2, :] * ext_ref[7:7 + t, :]
          + cw_ref[2:3, :] * ext_ref[8:8 + t, :] + cw_ref[3:4, :] * ext_ref[9:9 + t, :] + cb_ref[...])
    xcb = xc.astype(BF16)
    ng = D_RG // 256
    r_lin = jnp.concatenate([_dot(xcb[:, g * 256:(g + 1) * 256], wr_ref[g]) for g in range(ng)], axis=1)
    i_lin = jnp.concatenate([_dot(xcb[:, g * 256:(g + 1) * 256], wi_ref[g]) for g in range(ng)], axis=1)
    r = _sigmoid(r_lin + br_ref[...])
    i = _sigmoid(i_lin + bi_ref[...])
    log_a = (RG_C * r) * (-_softplus(-lam_ref[...]))
    a_ref[...] = jnp.exp(log_a)
    b_ref[...] = jnp.sqrt(1.0 - jnp.exp(2.0 * log_a)) * i * xc

    @pl.when(s == 0)
    def _():
        h_ref[...] = jnp.zeros_like(h_ref)

    def body(j, h):
        row = (t - 1 - j) if rev else j
        h = a_ref[pl.ds(row, 1), :] * h + b_ref[pl.ds(row, 1), :]
        b_ref[pl.ds(row, 1), :] = h
        return h

    h_ref[0:1, :] = lax.fori_loop(0, t, body, h_ref[0:1, :], unroll=8)
    if final:
        o_ref[...] = ((b_ref[...] + hf_ref[...]) * jax.nn.gelu(ga_ref[...])).astype(o_ref.dtype)
    else:
        o_ref[...] = b_ref[...]


def _rglru(p_rg, cw, cb, wr_bd, br, wi_bd, bi, lam, hf, *, rev):
    final = hf is not None
    t = SCAN_BLK
    hb = t // 8
    n8 = SEQ_ALL // 8
    row = lambda s: (_scan_blk(s, rev), 0)
    const2 = lambda s: (0, 0)
    in_specs = [pl.BlockSpec((t, D_RG), row),
                pl.BlockSpec((8, D_RG), lambda s: (jnp.maximum(_scan_blk(s, rev) * hb - 1, 0), 0)),
                pl.BlockSpec((8, D_RG), lambda s: (jnp.minimum(_scan_blk(s, rev) * hb + hb, n8 - 1), 0)),
                pl.BlockSpec((4, D_RG), const2),
                pl.BlockSpec((1, D_RG), const2),
                pl.BlockSpec((D_RG // 256, 256, 256), lambda s: (0, 0, 0)),
                pl.BlockSpec((1, D_RG), const2),
                pl.BlockSpec((D_RG // 256, 256, 256), lambda s: (0, 0, 0)),
                pl.BlockSpec((1, D_RG), const2),
                pl.BlockSpec((1, D_RG), const2)]
    args = [p_rg, p_rg, p_rg, cw, cb, wr_bd, br, wi_bd, bi, lam]
    if final:
        in_specs += [pl.BlockSpec((t, D_RG), lambda s: (_scan_blk(s, rev), 1)),
                     pl.BlockSpec((t, D_RG), lambda s: (_lat_blk(s, rev), 0))]
        args += [p_rg, hf]
    return pl.pallas_call(
        functools.partial(_rglru_kernel, rev=rev, final=final),
        grid=(N_SCAN_BLK,),
        in_specs=in_specs,
        out_specs=pl.BlockSpec((t, D_RG), lambda s: (_lat_blk(s, rev), 0)),
        out_shape=jax.ShapeDtypeStruct((SEQ, D_RG), BF16 if final else F32),
        scratch_shapes=[pltpu.VMEM((t + 16, D_RG), F32), pltpu.VMEM((t, D_RG), F32),
                        pltpu.VMEM((t, D_RG), F32), pltpu.VMEM((8, D_RG), F32)],
        compiler_params=_cp("arbitrary"),
        name="rglru_rev" if rev else "rglru_fwd",
    )(*args)


def _rg_blockdiag(w):
    w5 = w.reshape(2, RG_HEADS // 4, 4, 64, 64)
    eye = jnp.eye(4, dtype=w.dtype)
    return jnp.einsum('dgaij,ab->dgaibj', w5, eye).reshape(2, RG_HEADS // 4, 256, 256).astype(BF16)


PREP_BLK = 128
PREP_HALO = 64


def _rwprep_kernel(pm_ref, pp_ref, pn_ref, mu_ref, kkw_ref, ka_ref, rk_ref, w0_ref, w2_ref, a0_ref, a2_ref,
                   g2_ref, e_ref, r_ref, v_ref, kk_ref, bon_ref, g_ref,
                   ld0_ref, kd0_ref, b0_ref, ld1_ref, kd1_ref, b1_ref, ext_ref):
    t, hl = PREP_BLK, PREP_HALO
    row = pl.program_id(0) * t + lax.broadcasted_iota(jnp.int32, (t, 1), 0)
    col = lax.broadcasted_iota(jnp.int32, (1, RW_COLS_PAD), 1)
    ext_ref[0:hl, :] = pp_ref[...]
    ext_ref[hl:hl + t, :] = pm_ref[...]
    ext_ref[hl + t:2 * hl + t, :] = pn_ref[...]
    x = pm_ref[...]
    prev1 = ext_ref[hl - 1:hl - 1 + t, :]
    next1 = ext_ref[hl + 1:hl + 1 + t, :]
    up = ext_ref[0:t, :]
    down = ext_ref[2 * hl:2 * hl + t, :]
    is_ctx = row < CTX_LEN
    lat = row - CTX_LEN
    gcol = lat % GRID_W
    q = RW_COLS // 4
    left = jnp.where(gcol == 0, 0.0, prev1)
    right = jnp.where(gcol == GRID_W - 1, 0.0, next1)
    up = jnp.where(lat < GRID_W, 0.0, up)
    down = jnp.where(lat >= SEQ - GRID_W, 0.0, down)
    sh_lat = jnp.where(col < q, left, jnp.where(col < 2 * q, right, jnp.where(col < 3 * q, up, down)))
    prev_c = jnp.where(row == 0, 0.0, prev1)
    next_c = jnp.where(row == CTX_LEN - 1, 0.0, next1)
    sh_ctx = jnp.where(col < RW_COLS // 2, prev_c, next_c)
    shifted = jnp.where(is_ctx, sh_ctx, sh_lat)
    pb = x + (shifted - x) * mu_ref[...]

    r = pb[:, 0:D_RW]
    k = pb[:, D_RW:2 * D_RW]
    v = pb[:, 2 * D_RW:3 * D_RW]
    lo = pb[:, 3 * D_RW:3 * D_RW + 512]
    e = e_ref[...]
    kh = k * kkw_ref[...]
    kk = kh / jnp.maximum(jnp.sqrt(_split_dot(kh * kh, e)), 1e-12)
    r_ref[...] = r
    v_ref[...] = v
    kk_ref[...] = kk
    bon_ref[...] = _split_dot(r * k * rk_ref[...], e) * v
    g_ref[...] = _dot(_sigmoid(lo[:, 256:512]).astype(BF16), g2_ref[...])
    for d, (ld_ref, kd_ref, b_ref) in enumerate(((ld0_ref, kd0_ref, b0_ref), (ld1_ref, kd1_ref, b1_ref))):
        wlo = lo[:, d * RW_RANK:(d + 1) * RW_RANK]
        alo = lo[:, 128 + d * RW_RANK:128 + (d + 1) * RW_RANK]
        lw = w0_ref[d:d + 1, :] + _dot(jnp.tanh(wlo).astype(BF16), w2_ref[d])
        log_w = -_softplus(-lw) - 0.5
        ld_ref[...] = -jnp.exp(log_w)
        icl = _sigmoid(a0_ref[d:d + 1, :] + _dot(alo.astype(BF16), a2_ref[d]))
        kd_ref[...] = k * (1.0 + (icl - 1.0) * ka_ref[...])
        b_ref[...] = kk * icl


def _rwprep(p_rw, mu, kkw, ka, rk, w0, w2, a0, a2, g2, e):
    t, hl = PREP_BLK, PREP_HALO
    nb = SEQ_ALL // t
    nh = SEQ_ALL // hl
    per = t // hl
    c2 = lambda i: (0, 0)
    c3 = lambda i: (0, 0, 0)
    vec = pl.BlockSpec((1, D_RW), c2)
    in_specs = [pl.BlockSpec((t, RW_COLS_PAD), lambda i: (i, 0)),
                pl.BlockSpec((hl, RW_COLS_PAD), lambda i: (jnp.maximum(i * per - 1, 0), 0)),
                pl.BlockSpec((hl, RW_COLS_PAD), lambda i: (jnp.minimum(i * per + per, nh - 1), 0)),
                pl.BlockSpec((1, RW_COLS_PAD), c2), vec, vec, vec,
                pl.BlockSpec((2, D_RW), c2), pl.BlockSpec((2, RW_RANK, D_RW), c3),
                pl.BlockSpec((2, D_RW), c2), pl.BlockSpec((2, RW_RANK, D_RW), c3),
                pl.BlockSpec((256, D_RW), c2), pl.BlockSpec((D_RW, D_RW), c2)]
    out = jax.ShapeDtypeStruct((SEQ_ALL, D_RW), F32)
    return pl.pallas_call(
        _rwprep_kernel,
        grid=(nb,),
        in_specs=in_specs,
        out_specs=[pl.BlockSpec((t, D_RW), lambda i: (i, 0))] * 11,
        out_shape=[out] * 11,
        scratch_shapes=[pltpu.VMEM((t + 2 * hl, RW_COLS_PAD), F32)],
        compiler_params=_cp("parallel"),
        name="rwprep",
    )(p_rw, p_rw, p_rw, mu, kkw, ka, rk, w0, w2, a0, a2, g2, e)


def _head_ones():
    i = np.arange(D_RW) // RW_HEAD_DIM
    return jnp.asarray(i[:, None] == i[None, :], dtype=BF16)


def _cumsum_rows(x, rev):
    n = x.shape[0]
    row = lax.broadcasted_iota(jnp.int32, x.shape, 0)
    sh = 1
    while sh < n:
        if rev:
            x = x + jnp.where(row < n - sh, pltpu.roll(x, n - sh, 0), 0.0)
        else:
            x = x + jnp.where(row >= sh, pltpu.roll(x, sh, 0), 0.0)
        sh *= 2
    return x


def _wkv_local_kernel(r_ref, kk_ref, v_ref, ld_ref, kd_ref, b_ref, rp_ref, yl_ref, g_ref, h_ref, *, rev):
    c = WKV_C
    ii = lax.broadcasted_iota(jnp.int32, (2 * c, 2 * c), 0)
    jj = lax.broadcasted_iota(jnp.int32, (2 * c, 2 * c), 1)
    same = (ii // c) == (jj // c)
    strict = (jj > ii) if rev else (jj < ii)
    incl = (jj >= ii) if rev else (jj <= ii)
    eye = ii == jj

    def stack(x):
        return jnp.where(same, jnp.concatenate([x, x], axis=0), 0.0)

    for ci in range(SCAN_BLK // c):
        rows = slice(ci * c, (ci + 1) * c)
        ld = ld_ref[rows, :]
        cum = _cumsum_rows(ld, rev)
        tot = cum[0:1, :] if rev else cum[c - 1:c, :]
        g_in = jnp.exp(cum)
        g_ex = jnp.exp(cum - ld)
        g_inv = jnp.exp(-cum)
        g_end = jnp.exp(tot - cum)
        kk = kk_ref[rows, :]
        kd = kd_ref[rows, :]
        b = b_ref[rows, :]
        a_s = stack(-kk * g_ex)
        r_s = stack(r_ref[rows, :] * g_in)
        v_s = stack(v_ref[rows, :]).astype(BF16)
        lhs = jnp.concatenate([a_s, r_s], axis=0).astype(BF16)
        rhs = jnp.concatenate([stack(b * g_inv), stack(kd * g_inv)], axis=0).astype(BF16)
        p = _dot_nt(lhs, rhs)
        l_ab = jnp.where(strict, p[0:2 * c, 0:2 * c], 0.0)
        l_ak = jnp.where(strict, p[0:2 * c, 2 * c:4 * c], 0.0)
        m_rb = jnp.where(incl, p[2 * c:4 * c, 0:2 * c], 0.0)
        m_rk = jnp.where(incl, p[2 * c:4 * c, 2 * c:4 * c], 0.0)
        xb = l_ab.astype(BF16)
        tinv = jnp.where(eye, 1.0, 0.0) + l_ab
        for _ in range(5):
            x2 = _dot(xb, xb)
            xb = x2.astype(BF16)
            tinv = tinv + _dot(tinv.astype(BF16), xb)
        rhs2 = jnp.concatenate([a_s, _dot(l_ak.astype(BF16), v_s)], axis=1).astype(BF16)
        ta = _dot(tinv.astype(BF16), rhs2).astype(BF16)
        qb = _dot(m_rb.astype(BF16), ta)
        rp = r_s + qb[:, 0:2 * c]
        yl = qb[:, 2 * c:4 * c] + _dot(m_rk.astype(BF16), v_s)
        rp_ref[rows, :] = (rp[0:c, :] + rp[c:2 * c, :]).astype(rp_ref.dtype)
        yl_ref[rows, :] = yl[0:c, :] + yl[c:2 * c, :]
        bc_s = stack(b * g_end).astype(BF16)
        kc_s = stack(kd * g_end).astype(BF16)
        gp = _dot_tn(bc_s, ta)
        g_ref[ci, 0] = (jnp.where(eye, jnp.exp(tot), 0.0) + gp[:, 0:2 * c]).astype(g_ref.dtype)
        h_ref[ci, 0] = gp[:, 2 * c:4 * c] + _dot_tn(kc_s, v_s)


def _wkv_local(r, kk, v, ld, kd, b, *, rev):
    t = SCAN_BLK
    nc = t // WKV_C
    blk = pl.BlockSpec((t, 128), lambda i, j: (i, j))
    mat = pl.BlockSpec((nc, 1, 128, 128), lambda i, j: (i, j, 0, 0))
    n_chunk = SEQ_ALL // WKV_C
    return pl.pallas_call(
        functools.partial(_wkv_local_kernel, rev=rev),
        grid=(N_SCAN_BLK, N_PAIR),
        in_specs=[blk] * 6,
        out_specs=[blk, blk, mat, mat],
        out_shape=[jax.ShapeDtypeStruct((SEQ_ALL, D_RW), BF16), jax.ShapeDtypeStruct((SEQ_ALL, D_RW), F32),
                   jax.ShapeDtypeStruct((n_chunk, N_PAIR, 128, 128), BF16),
                   jax.ShapeDtypeStruct((n_chunk, N_PAIR, 128, 128), F32)],
        compiler_params=_cp("parallel", "parallel"),
        name="wkv_local_rev" if rev else "wkv_local_fwd",
    )(r, kk, v, ld, kd, b)


def _wkv_state_kernel(*refs, rev, final):
    if final:
        (rp_ref, yl_ref, g_ref, h_ref, yf_ref, bon_ref, gate_ref, lg_ref, lb_ref, e_ref, o_ref, st_ref, y_ref) = refs
    else:
        (rp_ref, yl_ref, g_ref, h_ref, o_ref, st_ref, y_ref) = refs
    c = WKV_C
    nc = SCAN_BLK // c

    @pl.when(pl.program_id(0) == 0)
    def _():
        st_ref[...] = jnp.zeros_like(st_ref)

    for k in range(nc):
        ci = (nc - 1 - k) if rev else k
        rows = slice(ci * c, (ci + 1) * c)
        for pr in range(N_PAIR):
            lanes = slice(pr * 128, (pr + 1) * 128)
            stb = st_ref[pr].astype(BF16)
            y_ref[rows, lanes] = _dot(rp_ref[rows, lanes], stb) + yl_ref[rows, lanes]
            st_ref[pr] = _dot(g_ref[ci, pr], stb) + h_ref[ci, pr]
    if final:
        y = y_ref[...] + yf_ref[...]
        e = e_ref[...]
        inv_n = 1.0 / RW_HEAD_DIM
        mu = _split_dot(y, e) * inv_n
        yc = y - mu
        var = _split_dot(yc * yc, e) * inv_n
        y_gn = yc * lax.rsqrt(var + RW_GN_EPS) * lg_ref[...] + lb_ref[...]
        o_ref[...] = ((y_gn + bon_ref[...]) * gate_ref[...]).astype(o_ref.dtype)
    else:
        o_ref[...] = y_ref[...]


def _wkv_state(rp, yl, g, h, fin, *, rev):
    final = fin is not None
    t = SCAN_BLK
    nc = t // WKV_C
    row = pl.BlockSpec((t, D_RW), lambda s: (_scan_blk(s, rev), 0))
    mat = pl.BlockSpec((nc, N_PAIR, 128, 128), lambda s: (_scan_blk(s, rev), 0, 0, 0))
    lat = pl.BlockSpec((t, D_RW), lambda s: (_lat_blk(s, rev), 0))
    in_specs = [row, row, mat, mat]
    args = [rp, yl, g, h]
    if final:
        yf, bon, gate, lg, lb, e = fin
        vec = pl.BlockSpec((1, D_RW), lambda s: (0, 0))
        in_specs += [lat, row, row, vec, vec, pl.BlockSpec((D_RW, D_RW), lambda s: (0, 0))]
        args += [yf, bon, gate, lg, lb, e]
    return pl.pallas_call(
        functools.partial(_wkv_state_kernel, rev=rev, final=final),
        grid=(N_SCAN_BLK,),
        in_specs=in_specs,
        out_specs=lat,
        out_shape=jax.ShapeDtypeStruct((SEQ, D_RW), BF16 if final else F32),
        scratch_shapes=[pltpu.VMEM((N_PAIR, 128, 128), F32), pltpu.VMEM((t, D_RW), F32)],
        compiler_params=_cp("arbitrary"),
        name="wkv_state_rev" if rev else "wkv_state_fwd",
    )(*args)


def _rw_weights(w):
    pad = RW_COLS_PAD - RW_COLS
    row = lambda a: a.reshape(1, -1)
    e = _head_ones()
    g2 = jnp.pad(w['rw_g2'], ((0, 256 - RW_GATE_RANK), (0, 0))).astype(BF16)
    prep = (jnp.pad(w['rw_mu'], (0, pad)).reshape(1, -1), row(w['rw_kk']), row(w['rw_ka']), row(w['rw_rk']),
            w['rw_w0'], w['rw_w2'].astype(BF16), w['rw_a0'], w['rw_a2'].astype(BF16), g2, e)
    return {'prep': prep, 'lnx_g': row(w['rw_lnx_g']), 'lnx_b': row(w['rw_lnx_b']), 'e': e}


def _res_ln(x, branch, gate, ln_g, ln_b):
    z = ALPHA * x + gate * branch
    mu = jnp.mean(z, axis=-1, keepdims=True)
    zc = z - mu
    var = jnp.mean(zc * zc, axis=-1, keepdims=True)
    return zc * lax.rsqrt(var + LN_EPS) * ln_g + ln_b


def _outproj_kernel(*refs, n_in):
    y_refs = refs[:n_in]
    w_refs = refs[n_in:2 * n_in]
    x_ref, gate_ref, lg_ref, lb_ref, o_ref = refs[2 * n_in:]
    acc = _dot(y_refs[0][...], w_refs[0][...])
    for y_ref, w_ref in zip(y_refs[1:], w_refs[1:]):
        acc = acc + _dot(y_ref[...], w_ref[...])
    o_ref[...] = _res_ln(x_ref[...], acc, gate_ref[0:1, :], lg_ref[...], lb_ref[...])


def _outproj(ys, ws, x, mod, gate_col, ln_g, ln_b, *, tm=512):
    m = x.shape[0]
    n_in = len(ys)
    vec = pl.BlockSpec((1, D_MODEL), lambda i: (0, 0))
    in_specs = ([pl.BlockSpec((tm, y.shape[1]), lambda i: (i, 0)) for y in ys]
                + [pl.BlockSpec(w.shape, lambda i: (0, 0)) for w in ws]
                + [pl.BlockSpec((tm, D_MODEL), lambda i: (i, 0)),
                   pl.BlockSpec((8, D_MODEL), lambda i: (0, gate_col)), vec, vec])
    return pl.pallas_call(
        functools.partial(_outproj_kernel, n_in=n_in),
        grid=(m // tm,),
        in_specs=in_specs,
        out_specs=pl.BlockSpec((tm, D_MODEL), lambda i: (i, 0)),
        out_shape=jax.ShapeDtypeStruct((m, D_MODEL), F32),
        compiler_params=_cp("parallel"),
        name="outproj",
    )(*ys, *ws, x, mod, ln_g.reshape(1, -1), ln_b.reshape(1, -1))


def _ffn_kernel(x_ref, sh_ref, sc_ref, gate_ref, wg_ref, wu_ref, wd_ref, lg_ref, lb_ref, o_ref, u_ref, acc_ref):
    f = pl.program_id(1)

    @pl.when(f == 0)
    def _():
        u_ref[...] = (x_ref[...] * (1.0 + sc_ref[0:1, :]) + sh_ref[0:1, :]).astype(BF16)
        acc_ref[...] = jnp.zeros_like(acc_ref)

    u = u_ref[...]
    hg = _dot(u, wg_ref[...])
    hu = _dot(u, wu_ref[...])
    acc_ref[...] += _dot((_silu(hg) * hu).astype(BF16), wd_ref[...])

    @pl.when(f == pl.num_programs(1) - 1)
    def _():
        o_ref[...] = _res_ln(x_ref[...], acc_ref[...], gate_ref[0:1, :], lg_ref[...], lb_ref[...])


def _ffn(x, mod, wg, wu, wd, ln_g, ln_b, *, tm=512, tf=512):
    m = x.shape[0]
    nf = wg.shape[1] // tf
    vec = pl.BlockSpec((1, D_MODEL), lambda i, f: (0, 0))
    modcol = lambda c: pl.BlockSpec((8, D_MODEL), lambda i, f: (0, c))
    return pl.pallas_call(
        _ffn_kernel,
        grid=(m // tm, nf),
        in_specs=[pl.BlockSpec((tm, D_MODEL), lambda i, f: (i, 0)), modcol(3), modcol(4), modcol(5),
                  pl.BlockSpec((D_MODEL, tf), lambda i, f: (0, f)),
                  pl.BlockSpec((D_MODEL, tf), lambda i, f: (0, f)),
                  pl.BlockSpec((tf, D_MODEL), lambda i, f: (f, 0)), vec, vec],
        out_specs=pl.BlockSpec((tm, D_MODEL), lambda i, f: (i, 0)),
        out_shape=jax.ShapeDtypeStruct((m, D_MODEL), F32),
        scratch_shapes=[pltpu.VMEM((tm, D_MODEL), BF16), pltpu.VMEM((tm, D_MODEL), F32)],
        compiler_params=_cp("parallel", "arbitrary"),
        name="ffn",
    )(x, mod, mod, mod, wg, wu, wd, ln_g.reshape(1, -1), ln_b.reshape(1, -1))


def _layer0(x, ctx, c_rows, w):
    mod = _ada(c_rows, w['mod0_w'], w['mod0_b'])
    xin = jnp.concatenate([ctx, x], axis=0)
    in_w = w['in0_w'].astype(BF16)
    p_rg = _inproj(xin, mod, in_w[:, :2 * D_RG], n_ctx=CTX_LEN, tm=768, tn=512)
    w_rw = jnp.pad(in_w[:, 2 * D_RG:], ((0, 0), (0, RW_COLS_PAD - RW_COLS)))
    p_rw = _inproj(xin, mod, w_rw, n_ctx=CTX_LEN, tm=768, tn=512)

    wr_bd, wi_bd = _rg_blockdiag(w['rg_wr']), _rg_blockdiag(w['rg_wi'])
    cb = w['rg_conv_b'].reshape(1, -1)
    rg = lambda d: (w['rg_conv_w'], cb, wr_bd[d], w['rg_br'][d:d + 1], wi_bd[d], w['rg_bi'][d:d + 1],
                    w['rg_lambda'][d:d + 1])
    h_f = _rglru(p_rg, *rg(0), None, rev=False)
    y_a = _rglru(p_rg, *rg(1), h_f, rev=True)

    rw = _rw_weights(w)
    r, v, kk, bon, gate, ld0, kd0, b0, ld1, kd1, b1 = _rwprep(p_rw, *rw['prep'])
    y_f = _wkv_state(*_wkv_local(r, kk, v, ld0, kd0, b0, rev=False), None, rev=False)
    y_b = _wkv_state(*_wkv_local(r, kk, v, ld1, kd1, b1, rev=True),
                     (y_f, bon, gate, rw['lnx_g'], rw['lnx_b'], rw['e']), rev=True)

    out_w = w['out0_w'].astype(BF16)
    x = _outproj([y_a, y_b], [out_w[:D_RG], out_w[D_RG:]], x, mod, 2, w['ln0a_g'], w['ln0a_b'])
    fpad = D_FF_PAD - D_FF
    wg = jnp.pad(w['ffn0_wg'].astype(BF16), ((0, 0), (0, fpad)))
    wu = jnp.pad(w['ffn0_wu'].astype(BF16), ((0, 0), (0, fpad)))
    wd = jnp.pad(w['ffn0_wd'].astype(BF16), ((0, fpad), (0, 0)))
    return _ffn(x, mod, wg, wu, wd, w['ln0b_g'], w['ln0b_b'])


def _dwconv3_kernel(xm_ref, xp_ref, xn_ref, w_ref, b_ref, o_ref, ext_ref):
    i = pl.program_id(0)
    t = xm_ref.shape[0]
    ext_ref[0:8, :] = jnp.where(i == 0, 0.0, xp_ref[...])
    ext_ref[8:8 + t, :] = xm_ref[...]
    ext_ref[8 + t:16 + t, :] = jnp.where(i == pl.num_programs(0) - 1, 0.0, xn_ref[...])
    o_ref[...] = (w_ref[0:1, :] * ext_ref[7:7 + t, :] + w_ref[1:2, :] * ext_ref[8:8 + t, :]
                  + w_ref[2:3, :] * ext_ref[9:9 + t, :] + b_ref[...])


def _dwconv3(p, w, b, *, tm=512, tn=1024):
    m, n = p.shape
    hb = tm // 8
    return pl.pallas_call(
        _dwconv3_kernel,
        grid=(m // tm, n // tn),
        in_specs=[pl.BlockSpec((tm, tn), lambda i, j: (i, j)),
                  pl.BlockSpec((8, tn), lambda i, j: (jnp.maximum(i * hb - 1, 0), j)),
                  pl.BlockSpec((8, tn), lambda i, j: (jnp.minimum(i * hb + hb, m // 8 - 1), j)),
                  pl.BlockSpec((3, tn), lambda i, j: (0, j)),
                  pl.BlockSpec((1, tn), lambda i, j: (0, j))],
        out_specs=pl.BlockSpec((tm, tn), lambda i, j: (i, j)),
        out_shape=jax.ShapeDtypeStruct((m, n), F32),
        scratch_shapes=[pltpu.VMEM((tm + 16, tn), F32)],
        compiler_params=_cp("parallel", "parallel"),
        name="dwconv3",
    )(p, p, p, w, b.reshape(1, n))


FFT_R = 128
FFT_N = FFT_R * FFT_R
FFT_H = FFT_R // 2
FFT_COLS = FFT_R * D_HY
FFT_TN = 8 * D_HY


def _dft_tables():
    ang = lambda ph, n: (2.0 * math.pi / n) * ph.astype(F32)
    k1 = jnp.arange(FFT_R, dtype=jnp.int32)
    n1 = jnp.arange(FFT_H, dtype=jnp.int32)
    pa = ang((k1[:, None] * n1[None, :]) % FFT_R, FFT_R)
    f1 = jnp.concatenate([jnp.cos(pa), -jnp.sin(pa)], axis=0)
    f3 = jnp.concatenate([jnp.cos(pa).T, -jnp.sin(pa).T], axis=1) * (1.0 / FFT_N)
    n2 = jnp.arange(FFT_R, dtype=jnp.int32)
    k2 = jnp.arange(FFT_R, dtype=jnp.int32)
    ph = (n2[None, None, :] * (k1[:, None, None] + FFT_R * k2[None, :, None])) % FFT_N
    cs, sn = jnp.cos(ang(ph, FFT_N)), jnp.sin(ang(ph, FFT_N))
    m = jnp.concatenate([jnp.concatenate([cs, sn], axis=2), jnp.concatenate([-sn, cs], axis=2)], axis=1)
    return f1.astype(BF16), f3.astype(BF16), m.astype(BF16), jnp.transpose(m, (0, 2, 1)).astype(BF16)


def _dft_a_kernel(f_ref, x_ref, o_ref):
    o_ref[...] = _dot(f_ref[...], x_ref[...].astype(BF16)).astype(o_ref.dtype)


def _dft_a(f1, x2d):
    return pl.pallas_call(
        _dft_a_kernel,
        grid=(FFT_COLS // FFT_TN,),
        in_specs=[pl.BlockSpec((2 * FFT_R, FFT_H), lambda j: (0, 0)),
                  pl.BlockSpec((FFT_H, FFT_TN), lambda j: (0, j))],
        out_specs=pl.BlockSpec((2 * FFT_R, FFT_TN), lambda j: (0, j)),
        out_shape=jax.ShapeDtypeStruct((2 * FFT_R, FFT_COLS), BF16),
        compiler_params=_cp("parallel"),
        name="dft_a",
    )(f1, x2d)


def _split3_dot(x, w):
    xh = x.astype(BF16)
    xl = (x - xh.astype(F32)).astype(BF16)
    wh = w.astype(BF16)
    wl = (w - wh.astype(F32)).astype(BF16)
    return _dot(xh, wh) + _dot(xl, wh) + _dot(xh, wl)


def _filter_a_kernel(band_ref, w1_ref, b1_ref, w2_ref, b2_ref, w3_ref, b3_ref, fr_ref, w4_ref, dl_ref, f_ref, o_ref):
    nb = pl.program_id(1)
    gi = pl.program_id(0)
    nq = FFT_TN // D_HY
    rows = nq * FFT_H
    ridx = lax.broadcasted_iota(jnp.int32, (rows, 1), 0)
    pos = FFT_R * (ridx % FFT_H) + nb * nq + ridx // FFT_H
    posf = pos.astype(F32)
    t = posf * (1.0 / (SEQ - 1))
    wv = posf * (2.0 * math.pi / SEQ)
    lane = lax.broadcasted_iota(jnp.int32, (1, 128), 1)
    arg = band_ref[...] * wv
    z = jnp.where(lane == 0, t, jnp.where(lane <= HY_BANDS, jnp.cos(arg),
                                          jnp.where(lane <= 2 * HY_BANDS, -jnp.sin(arg), 0.0)))
    fr = fr_ref[...]
    h = jnp.sin(fr * (_split3_dot(z, w1_ref[...]) + b1_ref[...]))
    h = jnp.sin(fr * (_split3_dot(h, w2_ref[...]) + b2_ref[...]))
    h = jnp.sin(fr * (_split3_dot(h, w3_ref[...]) + b3_ref[...]))
    kf = _split3_dot(h, w4_ref[...]) * jnp.exp(-t * dl_ref[...])
    kf = jnp.where((gi % 2 == 1) & (pos == 0), 0.0, kf).astype(BF16)
    f1 = f_ref[...]
    for q in range(nq):
        o_ref[0, :, q * D_HY:(q + 1) * D_HY] = _dot(f1, kf[q * FFT_H:(q + 1) * FFT_H, :]).astype(o_ref.dtype)


def _filter_a(w, f1):
    band = np.zeros((1, 128), np.float32)
    bands = np.linspace(1e-4, HY_BANDS - 1, HY_BANDS, dtype=np.float32)
    band[0, 1:1 + HY_BANDS] = bands
    band[0, 1 + HY_BANDS:1 + 2 * HY_BANDS] = bands
    w1 = jnp.pad(w['flt_w1'], ((0, 128 - HY_EMB), (0, 0)))
    dl = jnp.abs(jnp.linspace(HY_MIN_DECAY, HY_MAX_DECAY, D_HY, dtype=F32)).reshape(1, -1)
    row = lambda a: a.reshape(1, -1)
    c2 = lambda g, j: (0, 0)
    hid = pl.BlockSpec((1, HY_HIDDEN), c2)
    sq = pl.BlockSpec((HY_HIDDEN, HY_HIDDEN), c2)
    return pl.pallas_call(
        _filter_a_kernel,
        grid=(4, FFT_COLS // FFT_TN),
        in_specs=[pl.BlockSpec((1, 128), c2), pl.BlockSpec((128, HY_HIDDEN), c2), hid, sq, hid, sq, hid, hid,
                  pl.BlockSpec((HY_HIDDEN, D_HY), lambda g, j: (0, g)),
                  pl.BlockSpec((1, D_HY), c2), pl.BlockSpec((2 * FFT_R, FFT_H), c2)],
        out_specs=pl.BlockSpec((1, 2 * FFT_R, FFT_TN), lambda g, j: (g, 0, j)),
        out_shape=jax.ShapeDtypeStruct((4, 2 * FFT_R, FFT_COLS), BF16),
        compiler_params=_cp("parallel", "parallel"),
        name="filter_a",
    )(jnp.asarray(band), w1, row(w['flt_b1']), w['flt_w2'], row(w['flt_b2']), w['flt_w3'], row(w['flt_b3']),
      row(w['flt_freq']), w['flt_w4'], dl, f1)


def _filter_b_kernel(m_ref, af_ref, ab_ref, o_ref):
    m = m_ref[0]
    xf = _dot(m, jnp.concatenate([af_ref[0, 0, 0], af_ref[0, 1, 0]], axis=0))
    xb = _dot(m, jnp.concatenate([ab_ref[0, 0, 0], ab_ref[0, 1, 0]], axis=0))
    o_ref[0, 0] = jnp.concatenate([xf[0:FFT_R] + xb[0:FFT_R], xf[FFT_R:] - xb[FFT_R:]], axis=0).astype(o_ref.dtype)


def _filter_b(a5, m):
    blk = lambda g: pl.BlockSpec((1, 2, 1, FFT_R, D_HY), lambda o, k: (2 * o + g, 0, k, 0, 0))
    return pl.pallas_call(
        _filter_b_kernel,
        grid=(2, FFT_R),
        in_specs=[pl.BlockSpec((1, 2 * FFT_R, 2 * FFT_R), lambda o, k: (k, 0, 0)), blk(0), blk(1)],
        out_specs=pl.BlockSpec((1, 1, 2 * FFT_R, D_HY), lambda o, k: (o, k, 0, 0)),
        out_shape=jax.ShapeDtypeStruct((2, FFT_R, 2 * FFT_R, D_HY), BF16),
        compiler_params=_cp("parallel", "parallel"),
        name="filter_b",
    )(m, a5, a5)


def _spec_kernel(m_ref, mi_ref, a_ref, k_ref, o_ref):
    x = _dot(m_ref[0], jnp.concatenate([a_ref[0, 0], a_ref[1, 0]], axis=0))
    kf = k_ref[0, 0].astype(F32)
    xr, xi = x[0:FFT_R], x[FFT_R:]
    kr, ki = kf[0:FFT_R], kf[FFT_R:]
    y = jnp.concatenate([xr * kr - xi * ki, xr * ki + xi * kr], axis=0).astype(BF16)
    b = _dot(mi_ref[0], y)
    o_ref[0, 0] = b[0:FFT_R].astype(o_ref.dtype)
    o_ref[1, 0] = b[FFT_R:].astype(o_ref.dtype)


def _spec(a4, kspec, order, m, mi):
    mat = pl.BlockSpec((1, 2 * FFT_R, 2 * FFT_R), lambda k: (k, 0, 0))
    blk = pl.BlockSpec((2, 1, FFT_R, D_HY), lambda k: (0, k, 0, 0))
    return pl.pallas_call(
        _spec_kernel,
        grid=(FFT_R,),
        in_specs=[mat, mat, blk, pl.BlockSpec((1, 1, 2 * FFT_R, D_HY), lambda k: (order, k, 0, 0))],
        out_specs=blk,
        out_shape=jax.ShapeDtypeStruct((2, FFT_R, FFT_R, D_HY), BF16),
        compiler_params=_cp("parallel"),
        name="spec",
    )(m, mi, a4, kspec)


def _dft_c_kernel(f_ref, b_ref, gate_ref, z_ref, bias_ref, o_ref):
    conv = _dot(f_ref[...], b_ref[...])
    z = z_ref[...]
    o_ref[...] = (gate_ref[...] * (conv + bias_ref[...] * z)).astype(o_ref.dtype)


def _dft_c(f3, b2d, gate2d, z2d, bias, out_dtype):
    col = pl.BlockSpec((FFT_H, FFT_TN), lambda j: (0, j))
    return pl.pallas_call(
        _dft_c_kernel,
        grid=(FFT_COLS // FFT_TN,),
        in_specs=[pl.BlockSpec((FFT_H, 2 * FFT_R), lambda j: (0, 0)),
                  pl.BlockSpec((2 * FFT_R, FFT_TN), lambda j: (0, j)), col, col,
                  pl.BlockSpec((1, FFT_TN), lambda j: (0, 0))],
        out_specs=col,
        out_shape=jax.ShapeDtypeStruct((FFT_H, FFT_COLS), out_dtype),
        compiler_params=_cp("parallel"),
        name="dft_c",
    )(f3, b2d, gate2d, z2d, jnp.tile(bias, FFT_TN // D_HY).reshape(1, -1))


def _hyena(v, x1, x2, w):
    f1, f3, m, mi = _dft_tables()
    a_flt = _filter_a(w, f1)
    kspec = _filter_b(a_flt.reshape(4, 2, FFT_R, FFT_R, D_HY), m)
    view = lambda a: a.reshape(FFT_H, FFT_COLS)
    z = view(v)
    for o, gate in enumerate((x1, x2)):
        a4 = _dft_a(f1, z).reshape(2, FFT_R, FFT_R, D_HY)
        b4 = _spec(a4, kspec, o, m, mi)
        z = _dft_c(f3, b4.reshape(2 * FFT_R, FFT_COLS), view(gate), z, w['hy_bias'][o], F32 if o == 0 else BF16)
    return z.reshape(SEQ, D_HY)


def _router_kernel(x_ref, sh_ref, sc_ref, w_ref, b_ref, o_ref):
    u = x_ref[...] * (1.0 + sc_ref[0:1, :]) + sh_ref[0:1, :]
    lane = lax.broadcasted_iota(jnp.int32, (1, 128), 1)
    logits = jnp.where(lane < N_EXPERTS, _split3_dot(u, w_ref[...]) + b_ref[...], -1e30)
    m1 = jnp.max(logits, axis=-1, keepdims=True)
    i1 = jnp.min(jnp.where(logits == m1, lane, 128), axis=-1, keepdims=True)
    rest = jnp.where(lane == i1, -1e30, logits)
    m2 = jnp.max(rest, axis=-1, keepdims=True)
    i2 = jnp.min(jnp.where(rest == m2, lane, 128), axis=-1, keepdims=True)
    e2 = jnp.exp(m2 - m1)
    w1 = 1.0 / (1.0 + e2)
    o_ref[...] = jnp.where(lane == i1, w1, 0.0) + jnp.where(lane == i2, e2 * w1, 0.0)


def _router(x, mod, rw, rb, *, tm=512):
    m = x.shape[0]
    modcol = lambda c: pl.BlockSpec((8, D_MODEL), lambda i: (0, c))
    return pl.pallas_call(
        _router_kernel,
        grid=(m // tm,),
        in_specs=[pl.BlockSpec((tm, D_MODEL), lambda i: (i, 0)), modcol(3), modcol(4),
                  pl.BlockSpec((D_MODEL, 128), lambda i: (0, 0)), pl.BlockSpec((1, 128), lambda i: (0, 0))],
        out_specs=pl.BlockSpec((tm, 128), lambda i: (i, 0)),
        out_shape=jax.ShapeDtypeStruct((m, 128), F32),
        compiler_params=_cp("parallel"),
        name="router",
    )(x, mod, mod, jnp.pad(rw, ((0, 0), (0, 128 - N_EXPERTS))), jnp.pad(rb, (0, 128 - N_EXPERTS)).reshape(1, -1))


def _moe_kernel(x_ref, sh_ref, sc_ref, gate_ref, gw_ref, wg_ref, wu_ref, wd_ref, lg_ref, lb_ref, o_ref,
                u_ref, acc_ref, ge_ref):
    e = pl.program_id(1)
    f = pl.program_id(2)

    @pl.when((e == 0) & (f == 0))
    def _():
        u_ref[...] = (x_ref[...] * (1.0 + sc_ref[0:1, :]) + sh_ref[0:1, :]).astype(BF16)
        acc_ref[...] = jnp.zeros_like(acc_ref)

    @pl.when(f == 0)
    def _():
        lane = lax.broadcasted_iota(jnp.int32, (1, 128), 1)
        ge_ref[...] = jnp.sum(jnp.where(lane == e, gw_ref[...], 0.0), axis=-1, keepdims=True)

    u = u_ref[...]
    hg = _dot(u, wg_ref[0])
    hu = _dot(u, wu_ref[0])
    acc_ref[...] += _dot((ge_ref[...] * (_silu(hg) * hu)).astype(BF16), wd_ref[0])

    @pl.when((e == pl.num_programs(1) - 1) & (f == pl.num_programs(2) - 1))
    def _():
        o_ref[...] = _res_ln(x_ref[...], acc_ref[...], gate_ref[0:1, :], lg_ref[...], lb_ref[...])


def _moe(x, mod, gates, wg, wu, wd, ln_g, ln_b, *, tm=512, tf=512):
    m = x.shape[0]
    nf = wg.shape[2] // tf
    vec = pl.BlockSpec((1, D_MODEL), lambda i, e, f: (0, 0))
    modcol = lambda c: pl.BlockSpec((8, D_MODEL), lambda i, e, f: (0, c))
    return pl.pallas_call(
        _moe_kernel,
        grid=(m // tm, N_EXPERTS, nf),
        in_specs=[pl.BlockSpec((tm, D_MODEL), lambda i, e, f: (i, 0)), modcol(3), modcol(4), modcol(5),
                  pl.BlockSpec((tm, 128), lambda i, e, f: (i, 0)),
                  pl.BlockSpec((1, D_MODEL, tf), lambda i, e, f: (e, 0, f)),
                  pl.BlockSpec((1, D_MODEL, tf), lambda i, e, f: (e, 0, f)),
                  pl.BlockSpec((1, tf, D_MODEL), lambda i, e, f: (e, f, 0)), vec, vec],
        out_specs=pl.BlockSpec((tm, D_MODEL), lambda i, e, f: (i, 0)),
        out_shape=jax.ShapeDtypeStruct((m, D_MODEL), F32),
        scratch_shapes=[pltpu.VMEM((tm, D_MODEL), BF16), pltpu.VMEM((tm, D_MODEL), F32), pltpu.VMEM((tm, 1), F32)],
        compiler_params=_cp("parallel", "arbitrary", "arbitrary"),
        name="moe",
    )(x, mod, mod, mod, gates, wg, wu, wd, ln_g.reshape(1, -1), ln_b.reshape(1, -1))


def _layer1(x, c_rows, w):
    mod = _ada(c_rows, w['mod1_w'], w['mod1_b'])
    in_w = w['in1_w'].astype(BF16)
    branches = []
    for i in range(3):
        cols = slice(i * D_HY, (i + 1) * D_HY)
        p = _inproj(x, mod, in_w[:, cols], n_ctx=0, tm=512, tn=512)
        branches.append(_dwconv3(p, w['hy_conv_w'][:, cols], w['hy_conv_b'][cols]))
    z = _hyena(*branches, w)
    x = _outproj([z], [w['out1_w'].astype(BF16)], x, mod, 2, w['ln1a_g'], w['ln1a_b'])
    gates = _router(x, mod, w['router_w'], w['router_b'])
    return _moe(x, mod, gates, w['moe_wg'].astype(BF16), w['moe_wu'].astype(BF16), w['moe_wd'].astype(BF16),
                w['ln1b_g'], w['ln1b_b'])


def kernel(x, c, ctx, c_ctx, mod0_w, mod0_b, in0_w, rg_conv_w, rg_conv_b, rg_wr, rg_br, rg_wi, rg_bi, rg_lambda, rw_mu, rw_w0, rw_w2, rw_a0, rw_a2, rw_g2, rw_kk, rw_ka, rw_rk, rw_lnx_g, rw_lnx_b, out0_w, ln0a_g, ln0a_b, ffn0_wg, ffn0_wu, ffn0_wd, ln0b_g, ln0b_b, mod1_w, mod1_b, in1_w, hy_conv_w, hy_conv_b, flt_w1, flt_b1, flt_w2, flt_b2, flt_w3, flt_b3, flt_w4, flt_freq, hy_bias, out1_w, ln1a_g, ln1a_b, router_w, router_b, moe_wg, moe_wu, moe_wd, ln1b_g, ln1b_b):
    w = dict(locals())
    c_rows = jnp.zeros((8, D_MODEL), F32).at[0].set(c[0]).at[1].set(c_ctx)
    x1 = _layer0(x[0], ctx[0], c_rows, w)
    x2 = _layer1(x1, c_rows, w)
    return x2[None]
```
